```python
import math
import numpy as np
import jax
import jax.numpy as jnp
from jax import lax

D_MODEL = 2048
BATCH = 16
SEQ = 2048
DEPTH = 1

GLA_HEADS = 4
GLA_KW = D_MODEL // 2
GLA_VW = D_MODEL
GLA_DK = GLA_KW // GLA_HEADS
GLA_DV = GLA_VW // GLA_HEADS
GATE_RANK = 16
GATE_NORM = 16.0
CHUNK = 64
CONV_W = D_MODEL
CONV_K = 3
FFN_HIDDEN = 4 * D_MODEL
EPS = 1e-6

kernel_name = "hybrid_gla_shortconv_gated_merge"

_WIDTHS = (GLA_KW, GLA_KW, GLA_VW, GLA_VW, GATE_RANK, CONV_W, CONV_W, CONV_W, D_MODEL, D_MODEL)
IN_COLS = GLA_KW * 2 + GLA_VW * 2 + GATE_RANK + CONV_W * 3 + D_MODEL * 2


def rms_norm(x, g):
    xf = x.astype(jnp.float32)
    y = xf * lax.rsqrt(jnp.mean(xf * xf, axis=-1, keepdims=True) + EPS)
    return (y * g.astype(jnp.float32)).astype(x.dtype)


def _to_chunks(t):
    b, s, h, d = t.shape
    return jnp.transpose(t.reshape(b, s // CHUNK, CHUNK, h, d), (1, 0, 3, 2, 4))


def gla_chunked(q, k, v, log_a):
    b, s, h, dv = v.shape
    dk = q.shape[-1]
    qc = _to_chunks(q.astype(jnp.float32) * (dk ** -0.5))
    kc = _to_chunks(k.astype(jnp.float32))
    vc = _to_chunks(v.astype(jnp.float32))
    gc = _to_chunks(log_a.astype(jnp.float32))
    causal = jnp.tril(jnp.ones((CHUNK, CHUNK), dtype=bool))

    def step(state, inp):
        q_c, k_c, v_c, g_c = inp
        cum = jnp.cumsum(g_c, axis=2)
        last = cum[:, :, -1:, :]
        inter = jnp.einsum('bhtd,bhde->bhte', q_c * jnp.exp(cum), state)
        diff = cum[:, :, :, None, :] - cum[:, :, None, :, :]
        decay = jnp.exp(jnp.where(causal[None, None, :, :, None], diff, -jnp.inf))
        scores = jnp.einsum('bhtd,bhsd,bhtsd->bhts', q_c, k_c, decay)
        intra = jnp.einsum('bhts,bhse->bhte', scores, v_c)
        new_state = state * jnp.exp(last[:, :, 0, :, None]) + jnp.einsum(
            'bhcd,bhce->bhde', k_c * jnp.exp(last - cum), v_c)
        return new_state, inter + intra

    state0 = jnp.zeros((b, h, dk, dv), jnp.float32)
    _, out = lax.scan(step, state0, (qc, kc, vc, gc))
    return jnp.transpose(out, (1, 0, 3, 2, 4)).reshape(b, s, h, dv)


def causal_depthwise_conv(u, w):
    c = u.shape[-1]
    return lax.conv_general_dilated(
        u, w.reshape(CONV_K, 1, c).astype(u.dtype), window_strides=(1,),
        padding=((CONV_K - 1, 0),), dimension_numbers=('NWC', 'WIO', 'NWC'),
        feature_group_count=c)


def setup_inputs(seed: int = 0) -> dict:
    key = jax.random.key(seed)
    ks = jax.random.split(key, 20)
    n = jax.random.normal
    f = jnp.float32
    return {
        "x": n(ks[0], (BATCH, SEQ, D_MODEL), f),
        "g_mix": 1.0 + 0.02 * n(ks[1], (DEPTH, D_MODEL), f),
        "w_in": n(ks[2], (DEPTH, D_MODEL, IN_COLS), f) * D_MODEL ** -0.5,
        "w_gate_up": n(ks[3], (DEPTH, GATE_RANK, GLA_KW), f) * GATE_RANK ** -0.5,
        "b_gate": 0.02 * n(ks[4], (DEPTH, GLA_KW), f),
        "g_gla_norm": 1.0 + 0.02 * n(ks[5], (DEPTH, GLA_DV), f),
        "w_gla_out": n(ks[6], (DEPTH, GLA_VW, D_MODEL), f) * GLA_VW ** -0.5,
        "conv_w": n(ks[7], (DEPTH, CONV_K, CONV_W), f) * CONV_K ** -0.5,
        "w_conv_out": n(ks[8], (DEPTH, CONV_W, D_MODEL), f) * CONV_W ** -0.5,
        "w_o": n(ks[9], (DEPTH, D_MODEL, D_MODEL), f) * D_MODEL ** -0.5,
        "g_ffn": 1.0 + 0.02 * n(ks[10], (DEPTH, D_MODEL), f),
        "w_ffn_up": n(ks[11], (DEPTH, D_MODEL, FFN_HIDDEN), f) * D_MODEL ** -0.5,
        "w_ffn_down": n(ks[12], (DEPTH, FFN_HIDDEN, D_MODEL), f) * FFN_HIDDEN ** -0.5,
        "g_final": 1.0 + 0.02 * n(ks[13], (D_MODEL,), f),
    }


def reference(x, g_mix, w_in, w_gate_up, b_gate, g_gla_norm, w_gla_out, conv_w,
              w_conv_out, w_o, g_ffn, w_ffn_up, w_ffn_down, g_final):
    b, s, _ = x.shape
    splits = np.cumsum(_WIDTHS)[:-1].tolist()
    for l in range(DEPTH):
        h = rms_norm(x, g_mix[l])
        proj = jnp.einsum('bsd,dn->bsn', h, w_in[l])
        (q, k, v, r, lr, cb, cc, cx, ga, gb) = jnp.split(proj, splits, axis=-1)

        gate_pre = jnp.einsum('bsr,rk->bsk', lr, w_gate_up[l]) + b_gate[l]
        log_a = jax.nn.log_sigmoid(gate_pre.astype(jnp.float32)) / GATE_NORM
        o = gla_chunked(q.reshape(b, s, GLA_HEADS, GLA_DK),
                        k.reshape(b, s, GLA_HEADS, GLA_DK),
                        v.reshape(b, s, GLA_HEADS, GLA_DV),
                        log_a.reshape(b, s, GLA_HEADS, GLA_DK))
        o = rms_norm(o, g_gla_norm[l]).reshape(b, s, GLA_VW).astype(x.dtype)
        y_a = jnp.einsum('bse,ed->bsd', o * jax.nn.silu(r), w_gla_out[l])

        u = causal_depthwise_conv(cc * cx, conv_w[l])
        y_b = jnp.einsum('bsc,cd->bsd', cb * u, w_conv_out[l])

        merged = jax.nn.sigmoid(ga) * y_a + jax.nn.sigmoid(gb) * y_b
        x = x + jnp.einsum('bsd,de->bse', merged, w_o[l])

        h2 = rms_norm(x, g_ffn[l])
        hid = jnp.square(jax.nn.relu(jnp.einsum('bsd,df->bsf', h2, w_ffn_up[l])))
        x = x + jnp.einsum('bsf,fd->bsd', hid, w_ffn_down[l])
    return rms_norm(x, g_final)
```

```python
import functools

import jax
import jax.numpy as jnp
from jax import lax
from jax.experimental import pallas as pl
from jax.experimental.pallas import tpu as pltpu

F32 = jnp.float32
BF16 = jnp.bfloat16

D_MODEL = 2048
GLA_HEADS = 4
GLA_KW = D_MODEL // 2
GLA_VW = D_MODEL
GLA_DK = GLA_KW // GLA_HEADS
GLA_DV = GLA_VW // GLA_HEADS
GATE_RANK = 16
GATE_NORM = 16.0
CONV_K = 3
FFN_HIDDEN = 4 * D_MODEL
EPS = 1e-6

LANES = 128
CHUNK = 64
VMEM_LIMIT = 56 * 1024 * 1024


def _rms(x, g):
    y = x * lax.rsqrt(jnp.mean(x * x, axis=-1, keepdims=True) + EPS)
    return y * g


def _dot(a, b):
    return jnp.dot(a, b, preferred_element_type=F32)


def _dot_nt(a, b):
    return lax.dot_general(a, b, (((1,), (1,)), ((), ())), preferred_element_type=F32)


def _dot_tn(a, b):
    return lax.dot_general(a, b, (((0,), (0,)), ((), ())), preferred_element_type=F32)


def _params(sem):
    return pltpu.CompilerParams(dimension_semantics=sem, vmem_limit_bytes=VMEM_LIMIT)


def _in_proj_kernel(x_ref, g_ref, w_ref, wlr_ref, wgu_ref, bg_ref, proj_ref, loga_ref, h_ref):
    j = pl.program_id(1)

    @pl.when(j == 0)
    def _():
        h = _rms(x_ref[...], g_ref[...]).astype(BF16)
        h_ref[...] = h
        lr = _dot(h, wlr_ref[...]).astype(BF16)
        z = _dot(lr, wgu_ref[...]) + bg_ref[...]
        log_sig = jnp.minimum(z, 0.0) - jnp.log(1.0 + jnp.exp(-jnp.abs(z)))
        loga_ref[...] = log_sig * (1.0 / GATE_NORM)

    proj_ref[...] = _dot(h_ref[...], w_ref[...]).astype(BF16)


def _in_proj(x2, g, w_main, w_lr, w_gu, b_gate, tm=1024, tn=1024):
    t, d = x2.shape
    n = w_main.shape[1]
    return pl.pallas_call(
        _in_proj_kernel,
        grid=(t // tm, n // tn),
        in_specs=[
            pl.BlockSpec((tm, d), lambda i, j: (i, 0)),
            pl.BlockSpec((1, d), lambda i, j: (0, 0)),
            pl.BlockSpec((d, tn), lambda i, j: (0, j)),
            pl.BlockSpec((d, LANES), lambda i, j: (0, 0)),
            pl.BlockSpec((LANES, GLA_KW), lambda i, j: (0, 0)),
            pl.BlockSpec((1, GLA_KW), lambda i, j: (0, 0)),
        ],
        out_specs=[
            pl.BlockSpec((tm, tn), lambda i, j: (i, j)),
            pl.BlockSpec((tm, GLA_KW), lambda i, j: (i, 0)),
        ],
        out_shape=[
            jax.ShapeDtypeStruct((t, n), BF16),
            jax.ShapeDtypeStruct((t, GLA_KW), F32),
        ],
        scratch_shapes=[pltpu.VMEM((tm, d), BF16)],
        compiler_params=_params(("arbitrary", "arbitrary")),
        name="in_proj",
    )(x2, g, w_main, w_lr, w_gu, b_gate)


def _gla_kernel(q_ref, k_ref, v_ref, g_ref, r_ref, gn_ref, tri_ref, o_ref, st_ref, *, n_chunks):
    @pl.when(pl.program_id(2) == 0)
    def _():
        st_ref[...] = jnp.zeros_like(st_ref)

    tri = tri_ref[...]
    row = lax.broadcasted_iota(jnp.int32, (CHUNK, CHUNK), 0)
    col = lax.broadcasted_iota(jnp.int32, (CHUNK, CHUNK), 1)
    causal = row >= col
    gn = gn_ref[...]

    for c in range(n_chunks):
        sl = pl.ds(c * CHUNK, CHUNK)
        q = q_ref[sl, :].astype(F32) * (GLA_DK ** -0.5)
        k = k_ref[sl, :].astype(F32)
        v = v_ref[sl, :]
        g = g_ref[sl, :]
        g_hi = g.astype(BF16)
        g_lo = (g - g_hi.astype(F32)).astype(BF16)
        cum = _dot(tri, g_hi) + _dot(tri, g_lo)
        last = cum[CHUNK - 1:CHUNK, :]
        qe = (q * jnp.exp(cum)).astype(BF16)
        ki = (k * jnp.exp(-cum)).astype(BF16)
        kd = (k * jnp.exp(last - cum)).astype(BF16)
        scores = jnp.where(causal, _dot_nt(qe, ki), 0.0).astype(BF16)
        st = st_ref[...]
        o = _dot_nt(qe, st.astype(BF16)) + _dot(scores, v)
        st_ref[...] = st * jnp.exp(last) + _dot_tn(v, kd)
        on = _rms(o, gn)
        r = r_ref[sl, :].astype(F32)
        o_ref[sl, :] = (on * (r * jax.nn.sigmoid(r))).astype(BF16)


def _gla(proj, log_a, g_norm, tri, batch, seq, tc=256):
    t = proj.shape[0]
    nq = seq // tc
    kq = GLA_KW // GLA_DK
    kv = 2 * GLA_KW // GLA_DV
    kr = kv + GLA_VW // GLA_DV
    tok = lambda b, h, s: b * nq + s
    return pl.pallas_call(
        functools.partial(_gla_kernel, n_chunks=tc // CHUNK),
        grid=(batch, GLA_HEADS, nq),
        in_specs=[
            pl.BlockSpec((tc, GLA_DK), lambda b, h, s: (tok(b, h, s), h)),
            pl.BlockSpec((tc, GLA_DK), lambda b, h, s: (tok(b, h, s), kq + h)),
            pl.BlockSpec((tc, GLA_DV), lambda b, h, s: (tok(b, h, s), kv + h)),
            pl.BlockSpec((tc, GLA_DK), lambda b, h, s: (tok(b, h, s), h)),
            pl.BlockSpec((tc, GLA_DV), lambda b, h, s: (tok(b, h, s), kr + h)),
            pl.BlockSpec((1, GLA_DV), lambda b, h, s: (0, 0)),
            pl.BlockSpec((CHUNK, CHUNK), lambda b, h, s: (0, 0)),
        ],
        out_specs=pl.BlockSpec((tc, GLA_DV), lambda b, h, s: (tok(b, h, s), h)),
        out_shape=jax.ShapeDtypeStruct((t, GLA_VW), BF16),
        scratch_shapes=[pltpu.VMEM((GLA_DV, GLA_DK), F32)],
        compiler_params=_params(("arbitrary", "arbitrary", "arbitrary")),
        name="gla",
    )(proj, proj, proj, log_a, proj, g_norm, tri)


def _mix_kernel(og_ref, cb_ref, cc_ref, cx_ref, cch_ref, cxh_ref, ga_ref, gb_ref,
                cw_ref, wa_ref, wb_ref, o_ref, *, tiles_per_seq):
    tm = og_ref.shape[0]
    first = (pl.program_id(0) % tiles_per_seq) == 0
    p = cc_ref[...].astype(F32) * cx_ref[...].astype(F32)
    ph = cch_ref[...].astype(F32) * cxh_ref[...].astype(F32)
    ph = jnp.where(first, 0.0, ph)
    ext = jnp.concatenate([ph, p], axis=0)
    p1 = pltpu.roll(ext, 1, 0)[8:, :]
    p2 = pltpu.roll(ext, 2, 0)[8:, :]
    cw = cw_ref[...]
    u = cw[0:1, :] * p2 + cw[1:2, :] * p1 + cw[2:3, :] * p
    cbu = (cb_ref[...].astype(F32) * u).astype(BF16)
    ya = _dot(og_ref[...], wa_ref[...])
    yb = _dot(cbu, wb_ref[...])
    sa = jax.nn.sigmoid(ga_ref[...].astype(F32))
    sb = jax.nn.sigmoid(gb_ref[...].astype(F32))
    o_ref[...] = (sa * ya + sb * yb).astype(BF16)


def _mix(og, proj, conv_w, w_a, w_b, seq, tm=256):
    t, c = og.shape
    nb = t // tm
    cb0 = (2 * GLA_KW + 2 * GLA_VW) // c
    halo = lambda i: (jnp.maximum(i * (tm // 8) - 1, 0))
    tile = lambda col: pl.BlockSpec((tm, c), lambda i, col=col: (i, col))
    return pl.pallas_call(
        functools.partial(_mix_kernel, tiles_per_seq=seq // tm),
        grid=(nb,),
        in_specs=[
            pl.BlockSpec((tm, c), lambda i: (i, 0)),
            tile(cb0), tile(cb0 + 1), tile(cb0 + 2),
            pl.BlockSpec((8, c), lambda i: (halo(i), cb0 + 1)),
            pl.BlockSpec((8, c), lambda i: (halo(i), cb0 + 2)),
            tile(cb0 + 3), tile(cb0 + 4),
            pl.BlockSpec((CONV_K, c), lambda i: (0, 0)),
            pl.BlockSpec((c, c), lambda i: (0, 0)),
            pl.BlockSpec((c, c), lambda i: (0, 0)),
        ],
        out_specs=pl.BlockSpec((tm, c), lambda i: (i, 0)),
        out_shape=jax.ShapeDtypeStruct((t, c), BF16),
        compiler_params=_params(("arbitrary",)),
        name="mix",
    )(og, proj, proj, proj, proj, proj, proj, proj, conv_w, w_a, w_b)


def _out_proj_kernel(x_ref, m_ref, w_ref, o_ref):
    o_ref[...] = x_ref[...] + _dot(m_ref[...], w_ref[...])


def _out_proj(x2, merged, w_o, tm=512):
    t, d = x2.shape
    return pl.pallas_call(
        _out_proj_kernel,
        grid=(t // tm,),
        in_specs=[
            pl.BlockSpec((tm, d), lambda i: (i, 0)),
            pl.BlockSpec((tm, d), lambda i: (i, 0)),
            pl.BlockSpec((d, d), lambda i: (0, 0)),
        ],
        out_specs=pl.BlockSpec((tm, d), lambda i: (i, 0)),
        out_shape=jax.ShapeDtypeStruct((t, d), F32),
        compiler_params=_params(("arbitrary",)),
        name="out_proj",
    )(x2, merged, w_o)


def _ffn_kernel(x_ref, g_ref, wu_ref, wd_ref, gf_ref, o_ref, h_ref):
    f = pl.program_id(1)

    @pl.when(f == 0)
    def _():
        x = x_ref[...]
        h_ref[...] = _rms(x, g_ref[...]).astype(BF16)
        o_ref[...] = x

    a = jnp.maximum(_dot(h_ref[...], wu_ref[...]), 0.0)
    o_ref[...] += _dot((a * a).astype(BF16), wd_ref[...])

    @pl.when(f == pl.num_programs(1) - 1)
    def _():
        o_ref[...] = _rms(o_ref[...], gf_ref[...])


def _ffn(x1, g_ffn, w_up, w_down, g_final, tm=1024, tf=512):
    t, d = x1.shape
    hid = w_up.shape[1]
    return pl.pallas_call(
        _ffn_kernel,
        grid=(t // tm, hid // tf),
        in_specs=[
            pl.BlockSpec((tm, d), lambda i, f: (i, 0)),
            pl.BlockSpec((1, d), lambda i, f: (0, 0)),
            pl.BlockSpec((d, tf), lambda i, f: (0, f)),
            pl.BlockSpec((tf, d), lambda i, f: (f, 0)),
            pl.BlockSpec((1, d), lambda i, f: (0, 0)),
        ],
        out_specs=pl.BlockSpec((tm, d), lambda i, f: (i, 0)),
        out_shape=jax.ShapeDtypeStruct((t, d), F32),
        scratch_shapes=[pltpu.VMEM((tm, d), BF16)],
        compiler_params=_params(("arbitrary", "arbitrary")),
        name="ffn",
    )(x1, g_ffn, w_up, w_down, g_final)


def kernel(x, g_mix, w_in, w_gate_up, b_gate, g_gla_norm, w_gla_out, conv_w,
           w_conv_out, w_o, g_ffn, w_ffn_up, w_ffn_down, g_final):
    batch, seq, d = x.shape
    depth = w_in.shape[0]
    lr0 = 2 * GLA_KW + 2 * GLA_VW
    tri = jnp.tril(jnp.ones((CHUNK, CHUNK), F32)).astype(BF16)
    x2 = x.reshape(batch * seq, d)
    for l in range(depth):
        wi = w_in[l]
        w_main = jnp.concatenate([wi[:, :lr0], wi[:, lr0 + GATE_RANK:]], axis=1).astype(BF16)
        w_lr = jnp.pad(wi[:, lr0:lr0 + GATE_RANK], ((0, 0), (0, LANES - GATE_RANK))).astype(BF16)
        w_gu = jnp.pad(w_gate_up[l], ((0, LANES - GATE_RANK), (0, 0))).astype(BF16)
        proj, log_a = _in_proj(x2, g_mix[l][None, :], w_main, w_lr, w_gu, b_gate[l][None, :])
        og = _gla(proj, log_a, g_gla_norm[l][None, :], tri, batch, seq)
        merged = _mix(og, proj, conv_w[l], w_gla_out[l].astype(BF16),
                      w_conv_out[l].astype(BF16), seq)
        x1 = _out_proj(x2, merged, w_o[l].astype(BF16))
        is_last = l == depth - 1
        assert is_last, "final RMSNorm is fused into the last layer's MLP kernel"
        x2 = _ffn(x1, g_ffn[l][None, :], w_ffn_up[l].astype(BF16),
                  w_ffn_down[l].astype(BF16), g_final[None, :])
    return x2.reshape(batch, seq, d)
```

```python
import functools

import numpy as np
import jax
import jax.numpy as jnp
from jax import lax
from jax.experimental import pallas as pl
from jax.experimental.pallas import tpu as pltpu

F32 = jnp.float32
BF16 = jnp.bfloat16

D_MODEL = 2048
GLA_HEADS = 4
GLA_KW = D_MODEL // 2
GLA_VW = D_MODEL
GLA_DK = GLA_KW // GLA_HEADS
GLA_DV = GLA_VW // GLA_HEADS
GATE_RANK = 16
GATE_NORM = 16.0
CONV_K = 3
FFN_HIDDEN = 4 * D_MODEL
EPS = 1e-6

LANES = 128
SUBLANES = 8
VMEM_LIMIT = 56 * 1024 * 1024

GLA_BLOCK = 256
GLA_LEVELS = 8
SAFE_LOG_DECAY = -60.0


def _rms(x, g):
    y = x * lax.rsqrt(jnp.mean(x * x, axis=-1, keepdims=True) + EPS)
    return y * g


def _dot(a, b):
    return jnp.dot(a, b, preferred_element_type=F32)


def _dot_nt(a, b):
    return lax.dot_general(a, b, (((1,), (1,)), ((), ())), preferred_element_type=F32)


def _dot_tn(a, b):
    return lax.dot_general(a, b, (((0,), (0,)), ((), ())), preferred_element_type=F32)


def _params(sem):
    return pltpu.CompilerParams(dimension_semantics=sem, vmem_limit_bytes=VMEM_LIMIT)


def _in_proj_kernel(x_ref, g_ref, wl_ref, wr_ref, wlr_ref, wgu_ref, bg_ref,
                    proj_ref, loga_ref, dec_ref, h_ref, *, n_left):
    j = pl.program_id(1)

    @pl.when(j == 0)
    def _():
        h = _rms(x_ref[...], g_ref[...]).astype(BF16)
        h_ref[...] = h
        lr = _dot(h, wlr_ref[...]).astype(BF16)
        z = _dot(lr, wgu_ref[...]) + bg_ref[...]
        log_sig = jnp.minimum(z, 0.0) - jnp.log(1.0 + jnp.exp(-jnp.abs(z)))
        log_a = log_sig * (1.0 / GATE_NORM)
        loga_ref[...] = log_a
        rows = []
        for b in range(log_a.shape[0] // GLA_BLOCK):
            tot = jnp.sum(log_a[b * GLA_BLOCK:(b + 1) * GLA_BLOCK, :], axis=0, keepdims=True)
            rows.append(jnp.broadcast_to(jnp.min(tot, axis=1, keepdims=True), (1, LANES)))
        rows.append(jnp.zeros((SUBLANES - len(rows), LANES), F32))
        dec_ref[0] = jnp.concatenate(rows, axis=0)

    @pl.when(j < n_left)
    def _():
        proj_ref[...] = _dot(h_ref[...], wl_ref[...]).astype(BF16)

    @pl.when(j >= n_left)
    def _():
        proj_ref[...] = _dot(h_ref[...], wr_ref[...]).astype(BF16)


def _in_proj(x2, g, w_left, w_right, w_lr, w_gu, b_gate, tm=1024, tn=1024):
    t, d = x2.shape
    n_left = w_left.shape[1] // tn
    n = w_left.shape[1] + w_right.shape[1]
    assert tm // GLA_BLOCK <= SUBLANES
    return pl.pallas_call(
        functools.partial(_in_proj_kernel, n_left=n_left),
        grid=(t // tm, n // tn),
        in_specs=[
            pl.BlockSpec((tm, d), lambda i, j: (i, 0)),
            pl.BlockSpec((1, d), lambda i, j: (0, 0)),
            pl.BlockSpec((d, tn), lambda i, j: (0, jnp.minimum(j, n_left - 1))),
            pl.BlockSpec((d, tn), lambda i, j: (0, jnp.maximum(j - n_left, 0))),
            pl.BlockSpec((d, LANES), lambda i, j: (0, 0)),
            pl.BlockSpec((LANES, GLA_KW), lambda i, j: (0, 0)),
            pl.BlockSpec((1, GLA_KW), lambda i, j: (0, 0)),
        ],
        out_specs=[
            pl.BlockSpec((tm, tn), lambda i, j: (i, j)),
            pl.BlockSpec((tm, GLA_KW), lambda i, j: (i, 0)),
            pl.BlockSpec((1, SUBLANES, LANES), lambda i, j: (i, 0, 0)),
        ],
        out_shape=[
            jax.ShapeDtypeStruct((t, n), BF16),
            jax.ShapeDtypeStruct((t, GLA_KW), F32),
            jax.ShapeDtypeStruct((t // tm, SUBLANES, LANES), F32),
        ],
        scratch_shapes=[pltpu.VMEM((tm, d), BF16)],
        compiler_params=_params(("arbitrary", "arbitrary")),
        name="in_proj",
    )(x2, g, w_left, w_right, w_lr, w_gu, b_gate)


def _level_sum_matrices():
    t = np.arange(GLA_BLOCK)[:, None]
    s = np.arange(GLA_BLOCK)[None, :]
    mats = []
    for l in range(GLA_LEVELS):
        half = 1 << l
        upper = ((t >> l) & 1) == 1
        start_upper = (t >> l) << l
        end_lower = t | (half - 1)
        p = np.where(upper, (s >= start_upper) & (s <= t), (s > t) & (s <= end_lower))
        mats.append(p)
    return np.stack(mats).astype(np.float32)


def _gla_head(hd, fast, q_ref, k_ref, v_ref, g_ref, r_ref, gn_ref, tri_ref, lvl_ref, o_ref, st_ref):
    nb = GLA_BLOCK
    kc = slice(hd * GLA_DK, (hd + 1) * GLA_DK)
    vc = slice(hd * GLA_DV, (hd + 1) * GLA_DV)
    q = q_ref[:, kc].astype(F32) * (GLA_DK ** -0.5)
    k = k_ref[:, kc].astype(F32)
    v = v_ref[:, vc]
    g = g_ref[:, kc]
    g_hi = g.astype(BF16)
    g_lo = (g - g_hi.astype(F32)).astype(BF16)
    tri = tri_ref[...]
    cum = _dot(tri, g_hi) + _dot(tri, g_lo)
    last = cum[nb - 1:nb, :]
    qe = (q * jnp.exp(cum)).astype(BF16)
    kd = (k * jnp.exp(last - cum)).astype(BF16)
    row = lax.broadcasted_iota(jnp.int32, (nb, nb), 0)
    col = lax.broadcasted_iota(jnp.int32, (nb, nb), 1)
    if fast:
        ki = (k * jnp.exp(-cum)).astype(BF16)
        scores = jnp.where(row >= col, _dot_nt(qe, ki), 0.0)
    else:
        tok = lax.broadcasted_iota(jnp.int32, (nb, 1), 0)

        def level(l, acc):
            p = lvl_ref[l]
            e = jnp.exp(_dot(p, g_hi) + _dot(p, g_lo))
            upper = ((tok >> l) & 1) == 1
            qh = jnp.where(upper, q * e, 0.0).astype(BF16)
            kh = jnp.where(upper, 0.0, k * e).astype(BF16)
            same_parent = (row >> (l + 1)) == (col >> (l + 1))
            return acc + jnp.where(same_parent, _dot_nt(qh, kh), 0.0)

        diag = jnp.where(row == col, _dot_nt(q.astype(BF16), k.astype(BF16)), 0.0)
        scores = lax.fori_loop(0, GLA_LEVELS, level, diag)
    st = st_ref[hd]
    o = _dot_nt(qe, st.astype(BF16)) + _dot(scores.astype(BF16), v)
    st_ref[hd] = st * jnp.exp(last) + _dot_tn(v, kd)
    on = _rms(o, gn_ref[...])
    r = r_ref[:, vc].astype(F32)
    o_ref[:, vc] = (on * (r * jax.nn.sigmoid(r))).astype(BF16)


def _gla_kernel(safe_ref, q_ref, k_ref, v_ref, g_ref, r_ref, gn_ref, tri_ref, lvl_ref,
                o_ref, st_ref):
    b, s = pl.program_id(0), pl.program_id(1)

    @pl.when(s == 0)
    def _():
        st_ref[...] = jnp.zeros_like(st_ref)

    safe = safe_ref[b * pl.num_programs(1) + s]
    refs = (q_ref, k_ref, v_ref, g_ref, r_ref, gn_ref, tri_ref, lvl_ref, o_ref, st_ref)

    @pl.when(safe == 1)
    def _():
        for hd in range(GLA_HEADS):
            _gla_head(hd, True, *refs)

    @pl.when(safe == 0)
    def _():
        for hd in range(GLA_HEADS):
            _gla_head(hd, False, *refs)


def _gla(proj, log_a, safe, g_norm, batch, seq):
    t = proj.shape[0]
    nb = GLA_BLOCK
    nq = seq // nb
    tri = jnp.asarray(np.tril(np.ones((nb, nb), np.float32)), BF16)
    lvl = jnp.asarray(_level_sum_matrices(), BF16)
    tok = lambda b, s, safe_ref: b * nq + s
    grid_spec = pltpu.PrefetchScalarGridSpec(
        num_scalar_prefetch=1,
        grid=(batch, nq),
        in_specs=[
            pl.BlockSpec((nb, GLA_KW), lambda b, s, f: (tok(b, s, f), 0)),
            pl.BlockSpec((nb, GLA_KW), lambda b, s, f: (tok(b, s, f), 1)),
            pl.BlockSpec((nb, GLA_VW), lambda b, s, f: (tok(b, s, f), 1)),
            pl.BlockSpec((nb, GLA_KW), lambda b, s, f: (tok(b, s, f), 0)),
            pl.BlockSpec((nb, GLA_VW), lambda b, s, f: (tok(b, s, f), 2)),
            pl.BlockSpec((1, GLA_DV), lambda b, s, f: (0, 0)),
            pl.BlockSpec((nb, nb), lambda b, s, f: (0, 0)),
            pl.BlockSpec((GLA_LEVELS, nb, nb), lambda b, s, f: (0, 0, 0)),
        ],
        out_specs=pl.BlockSpec((nb, GLA_VW), lambda b, s, f: (tok(b, s, f), 0)),
        scratch_shapes=[pltpu.VMEM((GLA_HEADS, GLA_DV, GLA_DK), F32)],
    )
    return pl.pallas_call(
        _gla_kernel,
        grid_spec=grid_spec,
        out_shape=jax.ShapeDtypeStruct((t, GLA_VW), BF16),
        compiler_params=_params(("arbitrary", "arbitrary")),
        name="gla",
    )(safe, proj, proj, proj, log_a, proj, g_norm, tri, lvl)


def _mix_kernel(og_ref, cb_ref, cc_ref, cx_ref, cch_ref, cxh_ref, ga_ref, gb_ref,
                cw_ref, wa_ref, wb_ref, o_ref, *, tiles_per_seq):
    first = (pl.program_id(0) % tiles_per_seq) == 0
    p = cc_ref[...].astype(F32) * cx_ref[...].astype(F32)
    ph = cch_ref[...].astype(F32) * cxh_ref[...].astype(F32)
    ph = jnp.where(first, 0.0, ph)
    ext = jnp.concatenate([ph, p], axis=0)
    p1 = pltpu.roll(ext, 1, 0)[SUBLANES:, :]
    p2 = pltpu.roll(ext, 2, 0)[SUBLANES:, :]
    cw = cw_ref[...]
    u = cw[0:1, :] * p2 + cw[1:2, :] * p1 + cw[2:3, :] * p
    cbu = (cb_ref[...].astype(F32) * u).astype(BF16)
    ya = _dot(og_ref[...], wa_ref[...])
    yb = _dot(cbu, wb_ref[...])
    sa = jax.nn.sigmoid(ga_ref[...].astype(F32))
    sb = jax.nn.sigmoid(gb_ref[...].astype(F32))
    o_ref[...] = (sa * ya + sb * yb).astype(BF16)


def _mix(og, proj, conv_w, w_a, w_b, seq, tm=256):
    t, c = og.shape
    nb = t // tm
    cb0 = (2 * GLA_KW + 2 * GLA_VW) // c
    halo = lambda i: (jnp.maximum(i * (tm // SUBLANES) - 1, 0))
    tile = lambda col: pl.BlockSpec((tm, c), lambda i, col=col: (i, col))
    return pl.pallas_call(
        functools.partial(_mix_kernel, tiles_per_seq=seq // tm),
        grid=(nb,),
        in_specs=[
            pl.BlockSpec((tm, c), lambda i: (i, 0)),
            tile(cb0), tile(cb0 + 1), tile(cb0 + 2),
            pl.BlockSpec((SUBLANES, c), lambda i: (halo(i), cb0 + 1)),
            pl.BlockSpec((SUBLANES, c), lambda i: (halo(i), cb0 + 2)),
            tile(cb0 + 3), tile(cb0 + 4),
            pl.BlockSpec((CONV_K, c), lambda i: (0, 0)),
            pl.BlockSpec((c, c), lambda i: (0, 0)),
            pl.BlockSpec((c, c), lambda i: (0, 0)),
        ],
        out_specs=pl.BlockSpec((tm, c), lambda i: (i, 0)),
        out_shape=jax.ShapeDtypeStruct((t, c), BF16),
        compiler_params=_params(("arbitrary",)),
        name="mix",
    )(og, proj, proj, proj, proj, proj, proj, proj, conv_w, w_a, w_b)


def _out_proj_kernel(x_ref, m_ref, w_ref, o_ref):
    o_ref[...] = x_ref[...] + _dot(m_ref[...], w_ref[...])


def _out_proj(x2, merged, w_o, tm=512):
    t, d = x2.shape
    return pl.pallas_call(
        _out_proj_kernel,
        grid=(t // tm,),
        in_specs=[
            pl.BlockSpec((tm, d), lambda i: (i, 0)),
            pl.BlockSpec((tm, d), lambda i: (i, 0)),
            pl.BlockSpec((d, d), lambda i: (0, 0)),
        ],
        out_specs=pl.BlockSpec((tm, d), lambda i: (i, 0)),
        out_shape=jax.ShapeDtypeStruct((t, d), F32),
        compiler_params=_params(("arbitrary",)),
        name="out_proj",
    )(x2, merged, w_o)


def _ffn_kernel(x_ref, g_ref, wu_ref, wd_ref, gf_ref, o_ref, h_ref):
    f = pl.program_id(1)

    @pl.when(f == 0)
    def _():
        x = x_ref[...]
        h_ref[...] = _rms(x, g_ref[...]).astype(BF16)
        o_ref[...] = x

    a = jnp.maximum(_dot(h_ref[...], wu_ref[...]), 0.0)
    o_ref[...] += _dot((a * a).astype(BF16), wd_ref[...])

    @pl.when(f == pl.num_programs(1) - 1)
    def _():
        o_ref[...] = _rms(o_ref[...], gf_ref[...])


def _ffn(x1, g_ffn, w_up, w_down, g_final, tm=1024, tf=512):
    t, d = x1.shape
    hid = w_up.shape[1]
    return pl.pallas_call(
        _ffn_kernel,
        grid=(t // tm, hid // tf),
        in_specs=[
            pl.BlockSpec((tm, d), lambda i, f: (i, 0)),
            pl.BlockSpec((1, d), lambda i, f: (0, 0)),
            pl.BlockSpec((d, tf), lambda i, f: (0, f)),
            pl.BlockSpec((tf, d), lambda i, f: (f, 0)),
            pl.BlockSpec((1, d), lambda i, f: (0, 0)),
        ],
        out_specs=pl.BlockSpec((tm, d), lambda i, f: (i, 0)),
        out_shape=jax.ShapeDtypeStruct((t, d), F32),
        scratch_shapes=[pltpu.VMEM((tm, d), BF16)],
        compiler_params=_params(("arbitrary", "arbitrary")),
        name="ffn",
    )(x1, g_ffn, w_up, w_down, g_final)


def kernel(x, g_mix, w_in, w_gate_up, b_gate, g_gla_norm, w_gla_out, conv_w,
           w_conv_out, w_o, g_ffn, w_ffn_up, w_ffn_down, g_final):
    batch, seq, d = x.shape
    depth = w_in.shape[0]
    assert depth == 1, "the final RMSNorm is fused into the (single) layer's MLP kernel"
    lr0 = 2 * GLA_KW + 2 * GLA_VW
    x2 = x.reshape(batch * seq, d)
    wi = w_in[0]
    w_left = wi[:, :lr0].astype(BF16)
    w_right = wi[:, lr0 + GATE_RANK:].astype(BF16)
    w_lr = jnp.pad(wi[:, lr0:lr0 + GATE_RANK], ((0, 0), (0, LANES - GATE_RANK))).astype(BF16)
    w_gu = jnp.pad(w_gate_up[0], ((0, LANES - GATE_RANK), (0, 0))).astype(BF16)
    proj, log_a, dec = _in_proj(x2, g_mix[0][None, :], w_left, w_right, w_lr, w_gu,
                                b_gate[0][None, :])
    blocks_per_tile = (batch * seq // dec.shape[0]) // GLA_BLOCK
    safe = (dec[:, :blocks_per_tile, 0].reshape(-1) > SAFE_LOG_DECAY).astype(jnp.int32)
    og = _gla(proj, log_a, safe, g_gla_norm[0][None, :], batch, seq)
    merged = _mix(og, proj, conv_w[0], w_gla_out[0].astype(BF16), w_conv_out[0].astype(BF16), seq)
    x1 = _out_proj(x2, merged, w_o[0].astype(BF16))
    out = _ffn(x1, g_ffn[0][None, :], w_ffn_up[0].astype(BF16), w_ffn_down[0].astype(BF16),
               g_final[None, :])
    return out.reshape(batch, seq, d)
```

```python
import functools

import numpy as np
import jax
import jax.numpy as jnp
from jax import lax
from jax.experimental import pallas as pl
from jax.experimental.pallas import tpu as pltpu

F32 = jnp.float32
BF16 = jnp.bfloat16

D_MODEL = 2048
GLA_HEADS = 4
GLA_KW = D_MODEL // 2
GLA_VW = D_MODEL
GLA_DK = GLA_KW // GLA_HEADS
GLA_DV = GLA_VW // GLA_HEADS
GATE_RANK = 16
GATE_NORM = 16.0
CONV_K = 3
FFN_HIDDEN = 4 * D_MODEL
EPS = 1e-6

LANES = 128
SUBLANES = 8
VMEM_LIMIT = 56 * 1024 * 1024

GLA_BLOCK = 256
GLA_LEVELS = 8
SAFE_LOG_DECAY = -60.0
FFN_ROWS = 256


def _rms(x, g):
    y = x * lax.rsqrt(jnp.mean(x * x, axis=-1, keepdims=True) + EPS)
    return y * g


def _dot(a, b):
    return jnp.dot(a, b, preferred_element_type=F32)


def _dot_nt(a, b):
    return lax.dot_general(a, b, (((1,), (1,)), ((), ())), preferred_element_type=F32)


def _dot_tn(a, b):
    return lax.dot_general(a, b, (((0,), (0,)), ((), ())), preferred_element_type=F32)


def _params(sem):
    return pltpu.CompilerParams(dimension_semantics=sem, vmem_limit_bytes=VMEM_LIMIT)


def _in_proj_kernel(x_ref, g_ref, wl_ref, wr_ref, wlr_ref, wgu_ref, bg_ref,
                    proj_ref, loga_ref, dec_ref, h_ref, *, n_left):
    j = pl.program_id(1)
    tm = x_ref.shape[0]

    @pl.when(j == 0)
    def _():
        rows = []
        for c in range(tm // GLA_BLOCK):
            rs = pl.ds(c * GLA_BLOCK, GLA_BLOCK)
            h = _rms(x_ref[rs, :], g_ref[...]).astype(BF16)
            h_ref[rs, :] = h
            lr = _dot(h, wlr_ref[...]).astype(BF16)
            z = _dot(lr, wgu_ref[...]) + bg_ref[...]
            log_sig = jnp.minimum(z, 0.0) - jnp.log(1.0 + jnp.exp(-jnp.abs(z)))
            log_a = log_sig * (1.0 / GATE_NORM)
            loga_ref[rs, :] = log_a
            tot = jnp.sum(log_a, axis=0, keepdims=True)
            rows.append(jnp.broadcast_to(jnp.min(tot, axis=1, keepdims=True), (1, LANES)))
            proj_ref[rs, :] = _dot(h, wl_ref[...]).astype(BF16)
        rows.append(jnp.zeros((SUBLANES - len(rows), LANES), F32))
        dec_ref[0] = jnp.concatenate(rows, axis=0)

    @pl.when((j > 0) & (j < n_left))
    def _():
        proj_ref[...] = _dot(h_ref[...], wl_ref[...]).astype(BF16)

    @pl.when(j >= n_left)
    def _():
        proj_ref[...] = _dot(h_ref[...], wr_ref[...]).astype(BF16)


def _in_proj(x2, g, w_left, w_right, w_lr, w_gu, b_gate, tm=1024, tn=1024):
    t, d = x2.shape
    n_left = w_left.shape[1] // tn
    n_right = w_right.shape[1] // tn
    n = w_left.shape[1] + w_right.shape[1]
    assert tm // GLA_BLOCK <= SUBLANES
    return pl.pallas_call(
        functools.partial(_in_proj_kernel, n_left=n_left),
        grid=(t // tm, n // tn),
        in_specs=[
            pl.BlockSpec((tm, d), lambda i, j: (jnp.where(j > 0, jnp.minimum(i + 1, t // tm - 1), i), 0)),
            pl.BlockSpec((1, d), lambda i, j: (0, 0)),
            pl.BlockSpec((d, tn), lambda i, j: (0, jnp.where(j < n_left, j, 0))),
            pl.BlockSpec((d, tn), lambda i, j: (0, jnp.where(j < n_left, n_right - 1, j - n_left))),
            pl.BlockSpec((d, LANES), lambda i, j: (0, 0)),
            pl.BlockSpec((LANES, GLA_KW), lambda i, j: (0, 0)),
            pl.BlockSpec((1, GLA_KW), lambda i, j: (0, 0)),
        ],
        out_specs=[
            pl.BlockSpec((tm, tn), lambda i, j: (i, j)),
            pl.BlockSpec((tm, GLA_KW), lambda i, j: (i, 0)),
            pl.BlockSpec((1, SUBLANES, LANES), lambda i, j: (i, 0, 0)),
        ],
        out_shape=[
            jax.ShapeDtypeStruct((t, n), BF16),
            jax.ShapeDtypeStruct((t, GLA_KW), F32),
            jax.ShapeDtypeStruct((t // tm, SUBLANES, LANES), F32),
        ],
        scratch_shapes=[pltpu.VMEM((tm, d), BF16)],
        compiler_params=_params(("arbitrary", "arbitrary")),
        name="in_proj",
    )(x2, g, w_left, w_right, w_lr, w_gu, b_gate)


def _level_sum_matrices():
    t = np.arange(GLA_BLOCK)[:, None]
    s = np.arange(GLA_BLOCK)[None, :]
    mats = []
    for l in range(GLA_LEVELS):
        half = 1 << l
        upper = ((t >> l) & 1) == 1
        start_upper = (t >> l) << l
        end_lower = t | (half - 1)
        p = np.where(upper, (s >= start_upper) & (s <= t), (s > t) & (s <= end_lower))
        mats.append(p)
    return np.stack(mats).astype(np.float32)


def _gla_head(hd, fast, q_ref, k_ref, v_ref, g_ref, r_ref, gn_ref, tri_ref, lvl_ref, o_ref, st_ref):
    nb = GLA_BLOCK
    kc = slice(hd * GLA_DK, (hd + 1) * GLA_DK)
    vc = slice(hd * GLA_DV, (hd + 1) * GLA_DV)
    q = q_ref[:, kc].astype(F32)
    k = k_ref[:, kc].astype(F32)
    v = v_ref[:, vc]
    g = g_ref[:, kc]
    g_hi = g.astype(BF16)
    g_lo = (g - g_hi.astype(F32)).astype(BF16)
    tri = tri_ref[...]
    cum = _dot(tri, g_hi) + _dot(tri, g_lo)
    last = cum[nb - 1:nb, :]
    qe = (q * jnp.exp(cum)).astype(BF16)
    kd = (k * jnp.exp(last - cum)).astype(BF16)
    row = lax.broadcasted_iota(jnp.int32, (nb, nb), 0)
    col = lax.broadcasted_iota(jnp.int32, (nb, nb), 1)
    if fast:
        ki = (k * jnp.exp(-cum)).astype(BF16)
        scores = jnp.where(row >= col, _dot_nt(qe, ki), 0.0)
    else:
        tok = lax.broadcasted_iota(jnp.int32, (nb, 1), 0)

        def level(l, acc):
            p = lvl_ref[l]
            e = jnp.exp(_dot(p, g_hi) + _dot(p, g_lo))
            upper = ((tok >> l) & 1) == 1
            qh = jnp.where(upper, q * e, 0.0).astype(BF16)
            kh = jnp.where(upper, 0.0, k * e).astype(BF16)
            same_parent = (row >> (l + 1)) == (col >> (l + 1))
            return acc + jnp.where(same_parent, _dot_nt(qh, kh), 0.0)

        diag = jnp.where(row == col, _dot_nt(q.astype(BF16), k.astype(BF16)), 0.0)
        scores = lax.fori_loop(0, GLA_LEVELS, level, diag)
    st = st_ref[hd]
    o = _dot_nt(qe, st.astype(BF16)) + _dot(scores.astype(BF16), v)
    st_ref[hd] = st * jnp.exp(last) + _dot_tn(v, kd)
    on = _rms(o, gn_ref[...])
    r = r_ref[:, vc].astype(F32)
    o_ref[:, vc] = (on * (r * jax.nn.sigmoid(r))).astype(BF16)


def _gla_kernel(safe_ref, q_ref, k_ref, v_ref, g_ref, r_ref, gn_ref, tri_ref, lvl_ref,
                o_ref, st_ref):
    b, s = pl.program_id(0), pl.program_id(1)

    @pl.when(s == 0)
    def _():
        st_ref[...] = jnp.zeros_like(st_ref)

    safe = safe_ref[b * pl.num_programs(1) + s]
    refs = (q_ref, k_ref, v_ref, g_ref, r_ref, gn_ref, tri_ref, lvl_ref, o_ref, st_ref)

    @pl.when(safe == 1)
    def _():
        for hd in range(GLA_HEADS):
            _gla_head(hd, True, *refs)

    @pl.when(safe == 0)
    def _():
        for hd in range(GLA_HEADS):
            _gla_head(hd, False, *refs)


def _gla(proj, log_a, safe, g_norm, batch, seq):
    t = proj.shape[0]
    nb = GLA_BLOCK
    nq = seq // nb
    tri = jnp.asarray(np.tril(np.ones((nb, nb), np.float32)), BF16)
    lvl = jnp.asarray(_level_sum_matrices(), BF16)
    tok = lambda b, s, safe_ref: b * nq + s
    grid_spec = pltpu.PrefetchScalarGridSpec(
        num_scalar_prefetch=1,
        grid=(batch, nq),
        in_specs=[
            pl.BlockSpec((nb, GLA_KW), lambda b, s, f: (tok(b, s, f), 0)),
            pl.BlockSpec((nb, GLA_KW), lambda b, s, f: (tok(b, s, f), 1)),
            pl.BlockSpec((nb, GLA_VW), lambda b, s, f: (tok(b, s, f), 1)),
            pl.BlockSpec((nb, GLA_KW), lambda b, s, f: (tok(b, s, f), 0)),
            pl.BlockSpec((nb, GLA_VW), lambda b, s, f: (tok(b, s, f), 2)),
            pl.BlockSpec((1, GLA_DV), lambda b, s, f: (0, 0)),
            pl.BlockSpec((nb, nb), lambda b, s, f: (0, 0)),
            pl.BlockSpec((GLA_LEVELS, nb, nb), lambda b, s, f: (0, 0, 0)),
        ],
        out_specs=pl.BlockSpec((nb, GLA_VW), lambda b, s, f: (tok(b, s, f), 0)),
        scratch_shapes=[pltpu.VMEM((GLA_HEADS, GLA_DV, GLA_DK), F32)],
    )
    return pl.pallas_call(
        _gla_kernel,
        grid_spec=grid_spec,
        out_shape=jax.ShapeDtypeStruct((t, GLA_VW), BF16),
        compiler_params=_params(("arbitrary", "arbitrary")),
        name="gla",
    )(safe, proj, proj, proj, log_a, proj, g_norm, tri, lvl)


def _mix_kernel(og_ref, cb_ref, cc_ref, cx_ref, cch_ref, cxh_ref, ga_ref, gb_ref,
                cw_ref, wa_ref, wb_ref, o_ref, *, tiles_per_seq):
    first = (pl.program_id(0) % tiles_per_seq) == 0
    p = cc_ref[...].astype(F32) * cx_ref[...].astype(F32)
    ph = cch_ref[...].astype(F32) * cxh_ref[...].astype(F32)
    ph = jnp.where(first, 0.0, ph)
    ext = jnp.concatenate([ph, p], axis=0)
    p1 = pltpu.roll(ext, 1, 0)[SUBLANES:, :]
    p2 = pltpu.roll(ext, 2, 0)[SUBLANES:, :]
    cw = cw_ref[...]
    u = cw[0:1, :] * p2 + cw[1:2, :] * p1 + cw[2:3, :] * p
    cbu = (cb_ref[...].astype(F32) * u).astype(BF16)
    ya = _dot(og_ref[...], wa_ref[...])
    yb = _dot(cbu, wb_ref[...])
    sa = jax.nn.sigmoid(ga_ref[...].astype(F32))
    sb = jax.nn.sigmoid(gb_ref[...].astype(F32))
    o_ref[...] = (sa * ya + sb * yb).astype(BF16)


def _mix(og, proj, conv_w, w_a, w_b, seq, tm=256):
    t, c = og.shape
    nb = t // tm
    cb0 = (2 * GLA_KW + 2 * GLA_VW) // c
    halo = lambda i: (jnp.maximum(i * (tm // SUBLANES) - 1, 0))
    tile = lambda col: pl.BlockSpec((tm, c), lambda i, col=col: (i, col))
    return pl.pallas_call(
        functools.partial(_mix_kernel, tiles_per_seq=seq // tm),
        grid=(nb,),
        in_specs=[
            pl.BlockSpec((tm, c), lambda i: (i, 0)),
            tile(cb0), tile(cb0 + 1), tile(cb0 + 2),
            pl.BlockSpec((SUBLANES, c), lambda i: (halo(i), cb0 + 1)),
            pl.BlockSpec((SUBLANES, c), lambda i: (halo(i), cb0 + 2)),
            tile(cb0 + 3), tile(cb0 + 4),
            pl.BlockSpec((CONV_K, c), lambda i: (0, 0)),
            pl.BlockSpec((c, c), lambda i: (0, 0)),
            pl.BlockSpec((c, c), lambda i: (0, 0)),
        ],
        out_specs=pl.BlockSpec((tm, c), lambda i: (i, 0)),
        out_shape=jax.ShapeDtypeStruct((t, c), BF16),
        compiler_params=_params(("arbitrary",)),
        name="mix",
    )(og, proj, proj, proj, proj, proj, proj, proj, conv_w, w_a, w_b)


def _out_proj_kernel(x_ref, m_ref, w_ref, o_ref):
    o_ref[...] = x_ref[...] + _dot(m_ref[...], w_ref[...])


def _out_proj(x2, merged, w_o, tm=512):
    t, d = x2.shape
    return pl.pallas_call(
        _out_proj_kernel,
        grid=(t // tm,),
        in_specs=[
            pl.BlockSpec((tm, d), lambda i: (i, 0)),
            pl.BlockSpec((tm, d), lambda i: (i, 0)),
            pl.BlockSpec((d, d), lambda i: (0, 0)),
        ],
        out_specs=pl.BlockSpec((tm, d), lambda i: (i, 0)),
        out_shape=jax.ShapeDtypeStruct((t, d), F32),
        compiler_params=_params(("arbitrary",)),
        name="out_proj",
    )(x2, merged, w_o)


def _ffn_kernel(x_ref, g_ref, wu_ref, wd_ref, gf_ref, o_ref, h_ref):
    f = pl.program_id(1)
    last = pl.num_programs(1) - 1
    tm = x_ref.shape[0]

    def step(is_first, is_last):
        rows = FFN_ROWS if (is_first or is_last) else tm
        for c in range(tm // rows):
            rs = pl.ds(c * rows, rows)
            if is_first:
                acc = x_ref[rs, :]
                h = _rms(acc, g_ref[...]).astype(BF16)
                h_ref[rs, :] = h
            else:
                acc = o_ref[rs, :]
                h = h_ref[rs, :]
            a = jnp.maximum(_dot(h, wu_ref[...]), 0.0)
            acc = acc + _dot((a * a).astype(BF16), wd_ref[...])
            o_ref[rs, :] = _rms(acc, gf_ref[...]) if is_last else acc

    pl.when(f == 0)(functools.partial(step, True, False))
    pl.when((f > 0) & (f < last))(functools.partial(step, False, False))
    pl.when(f == last)(functools.partial(step, False, True))


def _ffn(x1, g_ffn, w_up, w_down, g_final, tm=1024, tf=512):
    t, d = x1.shape
    hid = w_up.shape[1]
    return pl.pallas_call(
        _ffn_kernel,
        grid=(t // tm, hid // tf),
        in_specs=[
            pl.BlockSpec((tm, d), lambda i, f: (jnp.where(f > 0, jnp.minimum(i + 1, t // tm - 1), i), 0)),
            pl.BlockSpec((1, d), lambda i, f: (0, 0)),
            pl.BlockSpec((d, tf), lambda i, f: (0, f)),
            pl.BlockSpec((tf, d), lambda i, f: (f, 0)),
            pl.BlockSpec((1, d), lambda i, f: (0, 0)),
        ],
        out_specs=pl.BlockSpec((tm, d), lambda i, f: (i, 0)),
        out_shape=jax.ShapeDtypeStruct((t, d), F32),
        scratch_shapes=[pltpu.VMEM((tm, d), BF16)],
        compiler_params=_params(("arbitrary", "arbitrary")),
        name="ffn",
    )(x1, g_ffn, w_up, w_down, g_final)


def kernel(x, g_mix, w_in, w_gate_up, b_gate, g_gla_norm, w_gla_out, conv_w,
           w_conv_out, w_o, g_ffn, w_ffn_up, w_ffn_down, g_final):
    batch, seq, d = x.shape
    depth = w_in.shape[0]
    assert depth == 1, "the final RMSNorm is fused into the (single) layer's MLP kernel"
    lr0 = 2 * GLA_KW + 2 * GLA_VW
    x2 = x.reshape(batch * seq, d)
    wi = w_in[0]
    q_scale = jnp.where(jnp.arange(lr0) < GLA_KW, GLA_DK ** -0.5, 1.0).astype(F32)
    w_left = (wi[:, :lr0] * q_scale[None, :]).astype(BF16)
    w_right = wi[:, lr0 + GATE_RANK:].astype(BF16)
    w_lr = jnp.pad(wi[:, lr0:lr0 + GATE_RANK], ((0, 0), (0, LANES - GATE_RANK))).astype(BF16)
    w_gu = jnp.pad(w_gate_up[0], ((0, LANES - GATE_RANK), (0, 0))).astype(BF16)
    proj, log_a, dec = _in_proj(x2, g_mix[0][None, :], w_left, w_right, w_lr, w_gu,
                                b_gate[0][None, :])
    blocks_per_tile = (batch * seq // dec.shape[0]) // GLA_BLOCK
    safe = (dec[:, :blocks_per_tile, 0].reshape(-1) > SAFE_LOG_DECAY).astype(jnp.int32)
    og = _gla(proj, log_a, safe, g_gla_norm[0][None, :], batch, seq)
    merged = _mix(og, proj, conv_w[0], w_gla_out[0].astype(BF16), w_conv_out[0].astype(BF16), seq)
    x1 = _out_proj(x2, merged, w_o[0].astype(BF16))
    out = _ffn(x1, g_ffn[0][None, :], w_ffn_up[0].astype(BF16), w_ffn_down[0].astype(BF16),
               g_final[None, :])
    return out.reshape(batch, seq, d)
```

```python
import functools

import numpy as np
import jax
import jax.numpy as jnp
from jax import lax
from jax.experimental import pallas as pl
from jax.experimental.pallas import tpu as pltpu

F32 = jnp.float32
BF16 = jnp.bfloat16

D_MODEL = 2048
GLA_HEADS = 4
GLA_KW = D_MODEL // 2
GLA_VW = D_MODEL
GLA_DK = GLA_KW // GLA_HEADS
GLA_DV = GLA_VW // GLA_HEADS
GATE_RANK = 16
GATE_NORM = 16.0
CONV_K = 3
FFN_HIDDEN = 4 * D_MODEL
EPS = 1e-6

LANES = 128
SUBLANES = 8
VMEM_LIMIT = 62 * 1024 * 1024

GLA_BLOCK = 256
GLA_LEVELS = 8
SAFE_LOG_DECAY = -60.0


def _rms(x, g):
    y = x * lax.rsqrt(jnp.mean(x * x, axis=-1, keepdims=True) + EPS)
    return y * g


def _dot(a, b):
    return jnp.dot(a, b, preferred_element_type=F32)


def _dot_nt(a, b):
    return lax.dot_general(a, b, (((1,), (1,)), ((), ())), preferred_element_type=F32)


def _dot_tn(a, b):
    return lax.dot_general(a, b, (((0,), (0,)), ((), ())), preferred_element_type=F32)


def _params(sem):
    return pltpu.CompilerParams(dimension_semantics=sem, vmem_limit_bytes=VMEM_LIMIT)


def _w_in_prep_kernel(w_ref, o_ref):
    o_ref[...] = w_ref[...].astype(BF16)


def _w_in_prep(wt, lr0, rows=1024):
    n_in, d = wt.shape
    n = n_in - GATE_RANK
    assert lr0 % rows == 0 and n % rows == 0
    src_row = lambda r: pl.multiple_of(
        jnp.where(r * rows < lr0, r * rows, r * rows + GATE_RANK), GATE_RANK)
    return pl.pallas_call(
        _w_in_prep_kernel,
        grid=(n // rows,),
        in_specs=[pl.BlockSpec((pl.Element(rows), pl.Element(d)), lambda r: (src_row(r), 0))],
        out_specs=pl.BlockSpec((rows, d), lambda r: (r, 0)),
        out_shape=jax.ShapeDtypeStruct((n, d), BF16),
        compiler_params=_params(("arbitrary",)),
        name="w_in_prep",
    )(wt)


def _in_proj_kernel(x_ref, g_ref, w_ref, wlr_ref, wgu_ref, bg_ref,
                    proj_ref, loga_ref, dec_ref, h_ref):
    j = pl.program_id(1)
    tm = x_ref.shape[0]

    @pl.when(j == 0)
    def _():
        rows = []
        for c in range(tm // GLA_BLOCK):
            rs = pl.ds(c * GLA_BLOCK, GLA_BLOCK)
            h = _rms(x_ref[rs, :], g_ref[...]).astype(BF16)
            h_ref[rs, :] = h
            lr = _dot_nt(h, wlr_ref[...]).astype(BF16)
            z = _dot(lr, wgu_ref[...]) + bg_ref[...]
            log_sig = jnp.minimum(z, 0.0) - jnp.log(1.0 + jnp.exp(-jnp.abs(z)))
            log_a = log_sig * (1.0 / GATE_NORM)
            loga_ref[rs, :] = log_a
            tot = jnp.sum(log_a, axis=0, keepdims=True)
            rows.append(jnp.broadcast_to(jnp.min(tot, axis=1, keepdims=True), (1, LANES)))
            if c % 2 == 1:
                ps = pl.ds((c - 1) * GLA_BLOCK, 2 * GLA_BLOCK)
                proj_ref[ps, :] = _dot_nt(h_ref[ps, :], w_ref[...]).astype(BF16)
        rows.append(jnp.zeros((SUBLANES - len(rows), LANES), F32))
        dec_ref[0] = jnp.concatenate(rows, axis=0)

    @pl.when(j > 0)
    def _():
        proj_ref[...] = _dot_nt(h_ref[...], w_ref[...]).astype(BF16)


def _in_proj(x2, g, wt_main, wt_lr, w_gu, b_gate, tm=1024, tn=2048):
    t, d = x2.shape
    n = wt_main.shape[0]
    assert tm // GLA_BLOCK <= SUBLANES and (tm // GLA_BLOCK) % 2 == 0
    return pl.pallas_call(
        _in_proj_kernel,
        grid=(t // tm, n // tn),
        in_specs=[
            pl.BlockSpec((tm, d), lambda i, j: (jnp.where(j > 0, jnp.minimum(i + 1, t // tm - 1), i), 0)),
            pl.BlockSpec((1, d), lambda i, j: (0, 0)),
            pl.BlockSpec((tn, d), lambda i, j: (j, 0)),
            pl.BlockSpec((LANES, d), lambda i, j: (0, 0)),
            pl.BlockSpec((LANES, GLA_KW), lambda i, j: (0, 0)),
            pl.BlockSpec((1, GLA_KW), lambda i, j: (0, 0)),
        ],
        out_specs=[
            pl.BlockSpec((tm, tn), lambda i, j: (i, j)),
            pl.BlockSpec((tm, GLA_KW), lambda i, j: (i, 0)),
            pl.BlockSpec((1, SUBLANES, LANES), lambda i, j: (i, 0, 0)),
        ],
        out_shape=[
            jax.ShapeDtypeStruct((t, n), BF16),
            jax.ShapeDtypeStruct((t, GLA_KW), F32),
            jax.ShapeDtypeStruct((t // tm, SUBLANES, LANES), F32),
        ],
        scratch_shapes=[pltpu.VMEM((tm, d), BF16)],
        compiler_params=_params(("arbitrary", "arbitrary")),
        name="in_proj",
    )(x2, g, wt_main, wt_lr, w_gu, b_gate)


def _level_sum_matrices():
    t = np.arange(GLA_BLOCK)[:, None]
    s = np.arange(GLA_BLOCK)[None, :]
    mats = []
    for l in range(GLA_LEVELS):
        half = 1 << l
        upper = ((t >> l) & 1) == 1
        start_upper = (t >> l) << l
        end_lower = t | (half - 1)
        p = np.where(upper, (s >= start_upper) & (s <= t), (s > t) & (s <= end_lower))
        mats.append(p)
    return np.stack(mats).astype(np.float32)


def _gla_head(hd, fast, q_ref, k_ref, v_ref, g_ref, r_ref, gn_ref, tri_ref, lvl_ref, o_ref, st_ref):
    nb = GLA_BLOCK
    kc = slice(hd * GLA_DK, (hd + 1) * GLA_DK)
    vc = slice(hd * GLA_DV, (hd + 1) * GLA_DV)
    q = q_ref[:, kc].astype(F32) * (GLA_DK ** -0.5)
    k = k_ref[:, kc].astype(F32)
    v = v_ref[:, vc]
    g = g_ref[:, kc]
    g_hi = g.astype(BF16)
    g_lo = (g - g_hi.astype(F32)).astype(BF16)
    tri = tri_ref[...]
    cum = _dot(tri, g_hi) + _dot(tri, g_lo)
    last = cum[nb - 1:nb, :]
    qe = (q * jnp.exp(cum)).astype(BF16)
    kd = (k * jnp.exp(last - cum)).astype(BF16)
    row = lax.broadcasted_iota(jnp.int32, (nb, nb), 0)
    col = lax.broadcasted_iota(jnp.int32, (nb, nb), 1)
    if fast:
        ki = (k * jnp.exp(-cum)).astype(BF16)
        scores = jnp.where(row >= col, _dot_nt(qe, ki), 0.0)
    else:
        tok = lax.broadcasted_iota(jnp.int32, (nb, 1), 0)

        def level(l, acc):
            p = lvl_ref[l]
            e = jnp.exp(_dot(p, g_hi) + _dot(p, g_lo))
            upper = ((tok >> l) & 1) == 1
            qh = jnp.where(upper, q * e, 0.0).astype(BF16)
            kh = jnp.where(upper, 0.0, k * e).astype(BF16)
            same_parent = (row >> (l + 1)) == (col >> (l + 1))
            return acc + jnp.where(same_parent, _dot_nt(qh, kh), 0.0)

        diag = jnp.where(row == col, _dot_nt(q.astype(BF16), k.astype(BF16)), 0.0)
        scores = lax.fori_loop(0, GLA_LEVELS, level, diag)
    st = st_ref[hd]
    o = _dot_nt(qe, st.astype(BF16)) + _dot(scores.astype(BF16), v)
    st_ref[hd] = st * jnp.exp(last) + _dot_tn(v, kd)
    on = _rms(o, gn_ref[...])
    r = r_ref[:, vc].astype(F32)
    o_ref[:, vc] = (on * (r * jax.nn.sigmoid(r))).astype(BF16)


def _gla_kernel(safe_ref, q_ref, k_ref, v_ref, g_ref, r_ref, gn_ref, tri_ref, lvl_ref,
                o_ref, st_ref):
    b, s = pl.program_id(0), pl.program_id(1)

    @pl.when(s == 0)
    def _():
        st_ref[...] = jnp.zeros_like(st_ref)

    safe = safe_ref[b * pl.num_programs(1) + s]
    refs = (q_ref, k_ref, v_ref, g_ref, r_ref, gn_ref, tri_ref, lvl_ref, o_ref, st_ref)

    @pl.when(safe == 1)
    def _():
        for hd in range(GLA_HEADS):
            _gla_head(hd, True, *refs)

    @pl.when(safe == 0)
    def _():
        for hd in range(GLA_HEADS):
            _gla_head(hd, False, *refs)


def _gla(proj, log_a, safe, g_norm, batch, seq):
    t = proj.shape[0]
    nb = GLA_BLOCK
    nq = seq // nb
    tri = jnp.asarray(np.tril(np.ones((nb, nb), np.float32)), BF16)
    lvl = jnp.asarray(_level_sum_matrices(), BF16)
    tok = lambda b, s, safe_ref: b * nq + s
    grid_spec = pltpu.PrefetchScalarGridSpec(
        num_scalar_prefetch=1,
        grid=(batch, nq),
        in_specs=[
            pl.BlockSpec((nb, GLA_KW), lambda b, s, f: (tok(b, s, f), 0)),
            pl.BlockSpec((nb, GLA_KW), lambda b, s, f: (tok(b, s, f), 1)),
            pl.BlockSpec((nb, GLA_VW), lambda b, s, f: (tok(b, s, f), 1)),
            pl.BlockSpec((nb, GLA_KW), lambda b, s, f: (tok(b, s, f), 0)),
            pl.BlockSpec((nb, GLA_VW), lambda b, s, f: (tok(b, s, f), 2)),
            pl.BlockSpec((1, GLA_DV), lambda b, s, f: (0, 0)),
            pl.BlockSpec((nb, nb), lambda b, s, f: (0, 0)),
            pl.BlockSpec((GLA_LEVELS, nb, nb), lambda b, s, f: (0, 0, 0)),
        ],
        out_specs=pl.BlockSpec((nb, GLA_VW), lambda b, s, f: (tok(b, s, f), 0)),
        scratch_shapes=[pltpu.VMEM((GLA_HEADS, GLA_DV, GLA_DK), F32)],
    )
    return pl.pallas_call(
        _gla_kernel,
        grid_spec=grid_spec,
        out_shape=jax.ShapeDtypeStruct((t, GLA_VW), BF16),
        compiler_params=_params(("arbitrary", "arbitrary")),
        name="gla",
    )(safe, proj, proj, proj, log_a, proj, g_norm, tri, lvl)


def _mix_kernel(og_ref, cb_ref, cc_ref, cx_ref, cch_ref, cxh_ref, ga_ref, gb_ref,
                cw_ref, wa_ref, wb_ref, o_ref, *, tiles_per_seq):
    first = (pl.program_id(0) % tiles_per_seq) == 0
    p = cc_ref[...].astype(F32) * cx_ref[...].astype(F32)
    ph = cch_ref[...].astype(F32) * cxh_ref[...].astype(F32)
    ph = jnp.where(first, 0.0, ph)
    ext = jnp.concatenate([ph, p], axis=0)
    p1 = pltpu.roll(ext, 1, 0)[SUBLANES:, :]
    p2 = pltpu.roll(ext, 2, 0)[SUBLANES:, :]
    cw = cw_ref[...]
    u = cw[0:1, :] * p2 + cw[1:2, :] * p1 + cw[2:3, :] * p
    cbu = (cb_ref[...].astype(F32) * u).astype(BF16)
    ya = _dot(og_ref[...], wa_ref[...])
    yb = _dot(cbu, wb_ref[...])
    sa = jax.nn.sigmoid(ga_ref[...].astype(F32))
    sb = jax.nn.sigmoid(gb_ref[...].astype(F32))
    o_ref[...] = (sa * ya + sb * yb).astype(BF16)


def _mix(og, proj, conv_w, w_a, w_b, seq, tm=512):
    t, c = og.shape
    nb = t // tm
    cb0 = (2 * GLA_KW + 2 * GLA_VW) // c
    halo = lambda i: (jnp.maximum(i * (tm // SUBLANES) - 1, 0))
    tile = lambda col: pl.BlockSpec((tm, c), lambda i, col=col: (i, col))
    return pl.pallas_call(
        functools.partial(_mix_kernel, tiles_per_seq=seq // tm),
        grid=(nb,),
        in_specs=[
            pl.BlockSpec((tm, c), lambda i: (i, 0)),
            tile(cb0), tile(cb0 + 1), tile(cb0 + 2),
            pl.BlockSpec((SUBLANES, c), lambda i: (halo(i), cb0 + 1)),
            pl.BlockSpec((SUBLANES, c), lambda i: (halo(i), cb0 + 2)),
            tile(cb0 + 3), tile(cb0 + 4),
            pl.BlockSpec((CONV_K, c), lambda i: (0, 0)),
            pl.BlockSpec((c, c), lambda i: (0, 0)),
            pl.BlockSpec((c, c), lambda i: (0, 0)),
        ],
        out_specs=pl.BlockSpec((tm, c), lambda i: (i, 0)),
        out_shape=jax.ShapeDtypeStruct((t, c), BF16),
        compiler_params=_params(("arbitrary",)),
        name="mix",
    )(og, proj, proj, proj, proj, proj, proj, proj, conv_w, w_a, w_b)


def _out_proj_kernel(x_ref, m_ref, w_ref, o_ref):
    o_ref[...] = x_ref[...] + _dot(m_ref[...], w_ref[...])


def _out_proj(x2, merged, w_o, tm=1024):
    t, d = x2.shape
    return pl.pallas_call(
        _out_proj_kernel,
        grid=(t // tm,),
        in_specs=[
            pl.BlockSpec((tm, d), lambda i: (i, 0)),
            pl.BlockSpec((tm, d), lambda i: (i, 0)),
            pl.BlockSpec((d, d), lambda i: (0, 0)),
        ],
        out_specs=pl.BlockSpec((tm, d), lambda i: (i, 0)),
        out_shape=jax.ShapeDtypeStruct((t, d), F32),
        compiler_params=_params(("arbitrary",)),
        name="out_proj",
    )(x2, merged, w_o)


def _ffn_kernel(x_ref, g_ref, wu_ref, wd_ref, gf_ref, o_ref, h_ref):
    f = pl.program_id(1)

    @pl.when(f == 0)
    def _():
        x = x_ref[...]
        h_ref[...] = _rms(x, g_ref[...]).astype(BF16)
        o_ref[...] = x

    a = jnp.maximum(_dot(h_ref[...], wu_ref[...]), 0.0)
    o_ref[...] += _dot((a * a).astype(BF16), wd_ref[...])

    @pl.when(f == pl.num_programs(1) - 1)
    def _():
        o_ref[...] = _rms(o_ref[...], gf_ref[...])


def _ffn(x1, g_ffn, w_up, w_down, g_final, tm=1024, tf=1024):
    t, d = x1.shape
    hid = w_up.shape[1]
    return pl.pallas_call(
        _ffn_kernel,
        grid=(t // tm, hid // tf),
        in_specs=[
            pl.BlockSpec((tm, d), lambda i, f: (i, 0)),
            pl.BlockSpec((1, d), lambda i, f: (0, 0)),
            pl.BlockSpec((d, tf), lambda i, f: (0, f)),
            pl.BlockSpec((tf, d), lambda i, f: (f, 0)),
            pl.BlockSpec((1, d), lambda i, f: (0, 0)),
        ],
        out_specs=pl.BlockSpec((tm, d), lambda i, f: (i, 0)),
        out_shape=jax.ShapeDtypeStruct((t, d), F32),
        scratch_shapes=[pltpu.VMEM((tm, d), BF16)],
        compiler_params=_params(("arbitrary", "arbitrary")),
        name="ffn",
    )(x1, g_ffn, w_up, w_down, g_final)


def kernel(x, g_mix, w_in, w_gate_up, b_gate, g_gla_norm, w_gla_out, conv_w,
           w_conv_out, w_o, g_ffn, w_ffn_up, w_ffn_down, g_final):
    batch, seq, d = x.shape
    depth = w_in.shape[0]
    assert depth == 1, "the final RMSNorm is fused into the (single) layer's MLP kernel"
    lr0 = 2 * GLA_KW + 2 * GLA_VW
    x2 = x.reshape(batch * seq, d)
    wt = jnp.transpose(w_in[0])
    wt_main = _w_in_prep(wt, lr0)
    wt_lr = jnp.pad(wt[lr0:lr0 + GATE_RANK], ((0, LANES - GATE_RANK), (0, 0))).astype(BF16)
    w_gu = jnp.pad(w_gate_up[0], ((0, LANES - GATE_RANK), (0, 0))).astype(BF16)
    proj, log_a, dec = _in_proj(x2, g_mix[0][None, :], wt_main, wt_lr, w_gu, b_gate[0][None, :])
    blocks_per_tile = (batch * seq // dec.shape[0]) // GLA_BLOCK
    safe = (dec[:, :blocks_per_tile, 0].reshape(-1) > SAFE_LOG_DECAY).astype(jnp.int32)
    og = _gla(proj, log_a, safe, g_gla_norm[0][None, :], batch, seq)
    merged = _mix(og, proj, conv_w[0], w_gla_out[0].astype(BF16), w_conv_out[0].astype(BF16), seq)
    x1 = _out_proj(x2, merged, w_o[0].astype(BF16))
    out = _ffn(x1, g_ffn[0][None, :], w_ffn_up[0].astype(BF16), w_ffn_down[0].astype(BF16),
               g_final[None, :])
    return out.reshape(batch, seq, d)
```
